```python
import jax, jax.numpy as jnp
from jax import lax
import numpy as np

D_MODEL = 1024
BATCH = 2
SEQ = 8192
DEPTH = 1

D_FF = 2816
D_A = D_MODEL
N_GROUPS_A = 8
CONV_K = 3
D_B = D_MODEL
N_HEADS_B = 8
DH_B = D_B // N_HEADS_B
CHUNK = 128
N_SUBLAYERS = 3
EPS = 1e-6
MACARON_W = 0.5
MIX_IN_WIDTHS = (D_MODEL, D_MODEL, D_A, D_A, D_A, D_B, D_B)
MIX_IN = sum(MIX_IN_WIDTHS)

kernel_name = "hybrid_conv_gmlp_macaron_adaln"


def rmsnorm(x, g):
    xf = x.astype(jnp.float32)
    y = xf * lax.rsqrt(jnp.mean(xf * xf, axis=-1, keepdims=True) + EPS)
    return (y * g.astype(jnp.float32)).astype(x.dtype)


def layernorm(x, g, b):
    xf = x.astype(jnp.float32)
    mu = jnp.mean(xf, axis=-1, keepdims=True)
    var = jnp.mean(jnp.square(xf - mu), axis=-1, keepdims=True)
    y = (xf - mu) * lax.rsqrt(var + EPS)
    return (y * g.astype(jnp.float32) + b.astype(jnp.float32)).astype(x.dtype)


def modulate(h, shift, scale):
    return h * (1.0 + scale[:, None, :]) + shift[:, None, :]


def swiglu(h, w_in, w_out):
    a, b = jnp.split(h @ w_in, 2, axis=-1)
    return (jax.nn.silu(a) * b) @ w_out


def causal_dwconv(z, w):
    k, ch = w.shape
    return lax.conv_general_dilated(
        z, w[:, None, :].astype(z.dtype), window_strides=(1,),
        padding=[(k - 1, 0)], dimension_numbers=("NWC", "WIO", "NWC"),
        feature_group_count=ch)


def chunked_spatial_gate(u, v, w_spatial, b_spatial):
    bsz, s, _ = v.shape
    mask = jnp.tril(jnp.ones((CHUNK, CHUNK), dtype=w_spatial.dtype))
    w_sp = w_spatial * mask[None]
    vh = v.reshape(bsz, s // CHUNK, CHUNK, N_HEADS_B, DH_B)
    sv = jnp.einsum("hts,bnshd->bnthd", w_sp, vh) + b_spatial.T[None, None, :, :, None]
    return u * sv.reshape(bsz, s, D_B)


def setup_inputs(seed: int = 0) -> dict:
    key = jax.random.key(seed)
    ks = jax.random.split(key, 24)
    f32 = jnp.float32
    n = lambda k, shape, s: jax.random.normal(k, shape, f32) * s
    gain = lambda k, d: 1.0 + 0.02 * jax.random.normal(k, (d,), f32)
    return {
        "x": jax.random.normal(ks[0], (BATCH, SEQ, D_MODEL), f32),
        "c": jax.random.normal(ks[1], (BATCH, D_MODEL), f32),
        "w_ada": n(ks[2], (D_MODEL, N_SUBLAYERS * 3 * D_MODEL), 0.5 * D_MODEL ** -0.5),
        "b_ada": n(ks[3], (N_SUBLAYERS * 3 * D_MODEL,), 0.02),
        "g_ffn1": gain(ks[4], D_MODEL),
        "w_ffn1_in": n(ks[5], (D_MODEL, 2 * D_FF), D_MODEL ** -0.5),
        "w_ffn1_out": n(ks[6], (D_FF, D_MODEL), D_FF ** -0.5),
        "g_mix": gain(ks[7], D_MODEL),
        "w_mix_in": n(ks[8], (D_MODEL, MIX_IN), D_MODEL ** -0.5),
        "conv_w": n(ks[9], (CONV_K, D_A), CONV_K ** -0.5),
        "ln_v_g": gain(ks[10], D_B),
        "ln_v_b": n(ks[11], (D_B,), 0.02),
        "w_spatial": n(ks[12], (N_HEADS_B, CHUNK, CHUNK), CHUNK ** -0.5),
        "b_spatial": 1.0 + n(ks[13], (N_HEADS_B, CHUNK), 0.02),
        "w_a_out": n(ks[14], (D_A, D_MODEL), D_A ** -0.5),
        "w_b_out": n(ks[15], (D_B, D_MODEL), D_B ** -0.5),
        "w_mix_out": n(ks[16], (D_MODEL, D_MODEL), D_MODEL ** -0.5),
        "g_ffn2": gain(ks[17], D_MODEL),
        "w_ffn2_in": n(ks[18], (D_MODEL, 2 * D_FF), D_MODEL ** -0.5),
        "w_ffn2_out": n(ks[19], (D_FF, D_MODEL), D_FF ** -0.5),
        "g_final": gain(ks[20], D_MODEL),
    }


def reference(x, c, w_ada, b_ada, g_ffn1, w_ffn1_in, w_ffn1_out, g_mix, w_mix_in,
              conv_w, ln_v_g, ln_v_b, w_spatial, b_spatial, w_a_out, w_b_out,
              w_mix_out, g_ffn2, w_ffn2_in, w_ffn2_out, g_final):
    split_pts = list(np.cumsum(MIX_IN_WIDTHS)[:-1])
    for _ in range(DEPTH):
        ada = jax.nn.silu(c) @ w_ada + b_ada
        (sh1, sc1, gt1, sh2, sc2, gt2, sh3, sc3, gt3) = jnp.split(ada, 3 * N_SUBLAYERS, axis=-1)

        h = modulate(rmsnorm(x, g_ffn1), sh1, sc1)
        x = x + MACARON_W * gt1[:, None, :] * swiglu(h, w_ffn1_in, w_ffn1_out)

        h = modulate(rmsnorm(x, g_mix), sh2, sc2)
        p = h @ w_mix_in
        gate_a, gate_b, b_a, c_a, x_a, u_b, v_b = jnp.split(p, split_pts, axis=-1)
        y_a = (b_a * causal_dwconv(c_a * x_a, conv_w)) @ w_a_out
        u_b = jax.nn.gelu(u_b)
        v_b = layernorm(jax.nn.gelu(v_b), ln_v_g, ln_v_b)
        y_b = chunked_spatial_gate(u_b, v_b, w_spatial, b_spatial) @ w_b_out
        m = (jax.nn.sigmoid(gate_a) * y_a + jax.nn.sigmoid(gate_b) * y_b) @ w_mix_out
        x = x + gt2[:, None, :] * m

        h = modulate(rmsnorm(x, g_ffn2), sh3, sc3)
        x = x + MACARON_W * gt3[:, None, :] * swiglu(h, w_ffn2_in, w_ffn2_out)
    return rmsnorm(x, g_final)
```

```python
import functools

import jax
import jax.numpy as jnp
from jax import lax
from jax.experimental import pallas as pl
from jax.experimental.pallas import tpu as pltpu

D_MODEL = 1024
D_FF = 2816
N_HEADS_B = 8
DH_B = 128
CHUNK = 128
CONV_K = 3
N_SUBLAYERS = 3
EPS = 1e-6
MACARON_W = 0.5

F32 = jnp.float32
BF16 = jnp.bfloat16

V7X_SUBLANES = 8
V7X_VMEM_BYTES = 64 * 1024 * 1024
VMEM_LIMIT_BYTES = V7X_VMEM_BYTES - 8 * 1024 * 1024

TM_FFN = 256
TM_MIX = 256
ADA_BLOCK_N = 1536


def _rms_scale(x):
    return lax.rsqrt(jnp.mean(x * x, axis=-1, keepdims=True) + EPS)


def _ada_kernel(ct_ref, w_ref, b_ref, o_ref, s_ref, *, batch):
    ct = ct_ref[...]
    s_ref[...] = ct * jax.nn.sigmoid(ct)
    nblk = w_ref.shape[1]
    n_k = w_ref.shape[0] // V7X_SUBLANES

    def body(k, accs):
        row = pl.multiple_of(k * V7X_SUBLANES, V7X_SUBLANES)
        w8 = w_ref[pl.ds(row, V7X_SUBLANES), :]
        s8 = s_ref[pl.ds(row, V7X_SUBLANES), :]
        return tuple(acc + w8 * s8[:, b:b + 1] for b, acc in enumerate(accs))

    init = tuple(jnp.zeros((V7X_SUBLANES, nblk), F32) for _ in range(batch))
    accs = lax.fori_loop(0, n_k, body, init, unroll=4)
    rows = [jnp.sum(acc, axis=0, keepdims=True) for acc in accs]
    o_ref[...] = jnp.concatenate(rows, axis=0) + b_ref[...]


def _ada_call(c, w_ada, b_ada):
    batch, d = c.shape
    n = w_ada.shape[1]
    return pl.pallas_call(
        functools.partial(_ada_kernel, batch=batch),
        out_shape=jax.ShapeDtypeStruct((batch, n), F32),
        grid=(n // ADA_BLOCK_N,),
        in_specs=[
            pl.BlockSpec((d, batch), lambda j: (0, 0)),
            pl.BlockSpec((d, ADA_BLOCK_N), lambda j: (0, j)),
            pl.BlockSpec((1, ADA_BLOCK_N), lambda j: (0, j)),
        ],
        out_specs=pl.BlockSpec((batch, ADA_BLOCK_N), lambda j: (0, j)),
        scratch_shapes=[pltpu.VMEM((d, batch), F32)],
        compiler_params=pltpu.CompilerParams(
            dimension_semantics=("arbitrary",),
            vmem_limit_bytes=VMEM_LIMIT_BYTES),
        name="ada_proj",
    )(c.T, w_ada, b_ada.reshape(1, n))


def _ffn_kernel(x_ref, ada_ref, g_ref, win_ref, wout_ref, *rest, final_norm):
    if final_norm:
        gf_ref, o_ref = rest
    else:
        (o_ref,) = rest
    x = x_ref[...]
    ada = ada_ref[...]
    shift, scale, gate = ada[0:1], ada[1:2], ada[2:3]
    h = (x * _rms_scale(x)) * (g_ref[...] * (1.0 + scale)) + shift
    hb = h.astype(BF16)
    a = jnp.dot(hb, win_ref[:, :D_FF], preferred_element_type=F32)
    b = jnp.dot(hb, win_ref[:, D_FF:], preferred_element_type=F32)
    act = (a * jax.nn.sigmoid(a) * b).astype(BF16)
    y = jnp.dot(act, wout_ref[...], preferred_element_type=F32)
    xn = x + (MACARON_W * gate) * y
    if final_norm:
        xn = (xn * _rms_scale(xn)) * gf_ref[...]
    o_ref[...] = xn


def _resident(shape):
    zeros = (0,) * len(shape)
    return pl.BlockSpec(shape, lambda b, i: zeros, pipeline_mode=pl.Buffered(1))


def _ffn_call(x, ada4, sub, g, w_in, w_out, g_final, name):
    bsz, s, d = x.shape
    tm = TM_FFN
    final_norm = g_final is not None
    tile = pl.BlockSpec((None, tm, d), lambda b, i: (b, i, 0))
    in_specs = [
        tile,
        pl.BlockSpec((None, None, N_SUBLAYERS, d), lambda b, i: (b, sub, 0, 0)),
        _resident((1, d)),
        _resident(w_in.shape),
        _resident(w_out.shape),
    ]
    args = [x, ada4, g.reshape(1, d), w_in, w_out]
    if final_norm:
        in_specs.append(_resident((1, d)))
        args.append(g_final.reshape(1, d))
    return pl.pallas_call(
        functools.partial(_ffn_kernel, final_norm=final_norm),
        out_shape=jax.ShapeDtypeStruct(x.shape, F32),
        grid=(bsz, s // tm),
        in_specs=in_specs,
        out_specs=tile,
        compiler_params=pltpu.CompilerParams(
            dimension_semantics=("arbitrary", "arbitrary"),
            vmem_limit_bytes=VMEM_LIMIT_BYTES),
        name=name,
    )(*args)


def _mix_kernel(x_ref, ada_ref, g_ref, win_ref, convw_ref, lng_ref, lnb_ref,
                wsp_ref, bsp_ref, wa_ref, wb_ref, wo_ref, o_ref,
                zbuf_ref, wspm_ref):
    d = D_MODEL
    tm = x_ref.shape[0]
    halo = V7X_SUBLANES
    first_tile = pl.program_id(1) == 0

    @pl.when(jnp.logical_and(pl.program_id(0) == 0, first_tile))
    def _():
        t_idx = lax.broadcasted_iota(jnp.int32, (CHUNK, CHUNK), 0)
        s_idx = lax.broadcasted_iota(jnp.int32, (CHUNK, CHUNK), 1)
        causal = (s_idx <= t_idx)[None]
        wspm_ref[...] = jnp.where(causal, wsp_ref[...], 0.0).astype(BF16)

    @pl.when(first_tile)
    def _():
        zbuf_ref[0:halo, :] = jnp.zeros((halo, d), F32)

    x = x_ref[...]
    ada = ada_ref[...]
    shift, scale, gate = ada[0:1], ada[1:2], ada[2:3]
    h = (x * _rms_scale(x)) * (g_ref[...] * (1.0 + scale)) + shift
    hb = h.astype(BF16)

    def proj(k):
        return jnp.dot(hb, win_ref[:, k * d:(k + 1) * d],
                       preferred_element_type=F32)

    z = proj(3) * proj(4)
    zbuf_ref[halo:halo + tm, :] = z
    cw = convw_ref[...]
    conv = (cw[0:1] * zbuf_ref[halo - 2:halo - 2 + tm, :]
            + cw[1:2] * zbuf_ref[halo - 1:halo - 1 + tm, :]
            + cw[2:3] * z)
    zbuf_ref[0:halo, :] = zbuf_ref[tm:tm + halo, :]
    y_a = jnp.dot((proj(2) * conv).astype(BF16), wa_ref[...],
                  preferred_element_type=F32)

    u = jax.nn.gelu(proj(5), approximate=True)
    gv = jax.nn.gelu(proj(6), approximate=True)
    mu = jnp.mean(gv, axis=-1, keepdims=True)
    cen = gv - mu
    var = jnp.mean(cen * cen, axis=-1, keepdims=True)
    v = (cen * lax.rsqrt(var + EPS)) * lng_ref[...] + lnb_ref[...]
    vb = v.astype(BF16)
    bias = bsp_ref[...]
    sv_rows = []
    for c in range(tm // CHUNK):
        r0 = c * CHUNK
        heads = []
        for hd in range(N_HEADS_B):
            c0 = hd * DH_B
            heads.append(jnp.dot(wspm_ref[hd], vb[r0:r0 + CHUNK, c0:c0 + DH_B],
                                 preferred_element_type=F32))
        sv_rows.append(jnp.concatenate(heads, axis=1) + bias)
    sv = jnp.concatenate(sv_rows, axis=0)
    y_b = jnp.dot((u * sv).astype(BF16), wb_ref[...],
                  preferred_element_type=F32)

    merged = jax.nn.sigmoid(proj(0)) * y_a + jax.nn.sigmoid(proj(1)) * y_b
    m = jnp.dot(merged.astype(BF16), wo_ref[...], preferred_element_type=F32)
    o_ref[...] = x + gate * m


def _mix_call(x, ada4, g, w_in, conv_w, ln_g, ln_b, w_sp, b_sp_full,
              w_a, w_b, w_o):
    bsz, s, d = x.shape
    tm = TM_MIX
    tile = pl.BlockSpec((None, tm, d), lambda b, i: (b, i, 0))
    in_specs = [
        tile,
        pl.BlockSpec((None, None, N_SUBLAYERS, d), lambda b, i: (b, 1, 0, 0)),
        _resident((1, d)),
        _resident(w_in.shape),
        _resident(conv_w.shape),
        _resident((1, d)),
        _resident((1, d)),
        _resident(w_sp.shape),
        _resident(b_sp_full.shape),
        _resident(w_a.shape),
        _resident(w_b.shape),
        _resident(w_o.shape),
    ]
    return pl.pallas_call(
        _mix_kernel,
        out_shape=jax.ShapeDtypeStruct(x.shape, F32),
        grid=(bsz, s // tm),
        in_specs=in_specs,
        out_specs=tile,
        scratch_shapes=[
            pltpu.VMEM((tm + V7X_SUBLANES, d), F32),
            pltpu.VMEM((N_HEADS_B, CHUNK, CHUNK), BF16),
        ],
        compiler_params=pltpu.CompilerParams(
            dimension_semantics=("arbitrary", "arbitrary"),
            vmem_limit_bytes=VMEM_LIMIT_BYTES),
        name="mix",
    )(x, ada4, g.reshape(1, d), w_in, conv_w, ln_g.reshape(1, d),
      ln_b.reshape(1, d), w_sp, b_sp_full, w_a, w_b, w_o)


def kernel(x, c, w_ada, b_ada, g_ffn1, w_ffn1_in, w_ffn1_out, g_mix, w_mix_in,
           conv_w, ln_v_g, ln_v_b, w_spatial, b_spatial, w_a_out, w_b_out,
           w_mix_out, g_ffn2, w_ffn2_in, w_ffn2_out, g_final):
    bsz, _, d = x.shape
    ada = _ada_call(c, w_ada, b_ada)
    ada4 = ada.reshape(bsz, N_SUBLAYERS, 3, d)
    b_sp_full = jnp.repeat(b_spatial.T, DH_B, axis=1)
    bf = lambda w: w.astype(BF16)

    x = _ffn_call(x, ada4, 0, g_ffn1, bf(w_ffn1_in), bf(w_ffn1_out), None,
                  "ffn1")
    x = _mix_call(x, ada4, g_mix, bf(w_mix_in), conv_w, ln_v_g, ln_v_b,
                  w_spatial, b_sp_full, bf(w_a_out), bf(w_b_out), bf(w_mix_out))
    x = _ffn_call(x, ada4, 2, g_ffn2, bf(w_ffn2_in), bf(w_ffn2_out), g_final,
                  "ffn2_final")
    return x
```

```python
import functools

import jax
import jax.numpy as jnp
from jax import lax
from jax.experimental import pallas as pl
from jax.experimental.pallas import tpu as pltpu

D_MODEL = 1024
D_FF = 2816
N_HEADS_B = 8
DH_B = 128
CHUNK = 128
CONV_K = 3
N_SUBLAYERS = 3
EPS = 1e-6
MACARON_W = 0.5

F32 = jnp.float32
BF16 = jnp.bfloat16

V7X_SUBLANES = 8
V7X_LANES = 128
V7X_VMEM_BYTES = 64 * 1024 * 1024
VMEM_LIMIT_BYTES = V7X_VMEM_BYTES - 8 * 1024 * 1024

TM_FFN = 512
TM_MIX = 512
ADA_BLOCK_N = 768


def _rms_scale(x):
    return lax.rsqrt(jnp.mean(x * x, axis=-1, keepdims=True) + EPS)


def _ada_kernel(ct_ref, w_ref, b_ref, o_ref, s_ref, *, batch):
    d = w_ref.shape[0]
    nblk = w_ref.shape[1]
    n_k = d // V7X_SUBLANES

    @pl.when(pl.program_id(0) == 0)
    def _():
        ct = ct_ref[...]
        s = ct * jax.nn.sigmoid(ct)
        for b in range(batch):
            s_ref[b] = jnp.broadcast_to(s[:, b:b + 1], (d, V7X_LANES))

    def body(k, accs):
        row = pl.multiple_of(k * V7X_SUBLANES, V7X_SUBLANES)
        w8 = w_ref[pl.ds(row, V7X_SUBLANES), :]
        out = []
        for b, acc in enumerate(accs):
            s8 = s_ref[b, pl.ds(row, V7X_SUBLANES), :]
            out.append(acc + w8 * jnp.tile(s8, (1, nblk // V7X_LANES)))
        return tuple(out)

    init = tuple(jnp.zeros((V7X_SUBLANES, nblk), F32) for _ in range(batch))
    accs = lax.fori_loop(0, n_k, body, init, unroll=8)
    rows = [jnp.sum(acc, axis=0, keepdims=True) for acc in accs]
    o_ref[...] = jnp.concatenate(rows, axis=0) + b_ref[...]


def _ada_call(c, w_ada, b_ada):
    batch, d = c.shape
    n = w_ada.shape[1]
    return pl.pallas_call(
        functools.partial(_ada_kernel, batch=batch),
        out_shape=jax.ShapeDtypeStruct((batch, n), F32),
        grid=(n // ADA_BLOCK_N,),
        in_specs=[
            pl.BlockSpec((d, batch), lambda j: (0, 0)),
            pl.BlockSpec((d, ADA_BLOCK_N), lambda j: (0, j)),
            pl.BlockSpec((1, ADA_BLOCK_N), lambda j: (0, j)),
        ],
        out_specs=pl.BlockSpec((batch, ADA_BLOCK_N), lambda j: (0, j)),
        scratch_shapes=[pltpu.VMEM((batch, d, V7X_LANES), F32)],
        compiler_params=pltpu.CompilerParams(
            dimension_semantics=("arbitrary",),
            vmem_limit_bytes=VMEM_LIMIT_BYTES),
        name="ada_proj",
    )(c.T, w_ada, b_ada.reshape(1, n))


def _ffn_kernel(x_ref, ada_ref, g_ref, win_ref, wout_ref, *rest, final_norm):
    if final_norm:
        gf_ref, o_ref = rest
    else:
        (o_ref,) = rest
    x = x_ref[...]
    ada = ada_ref[...]
    shift, scale, gate = ada[0:1], ada[1:2], ada[2:3]
    h = (x * _rms_scale(x)) * (g_ref[...] * (1.0 + scale)) + shift
    hb = h.astype(BF16)
    a = jnp.dot(hb, win_ref[:, :D_FF], preferred_element_type=F32)
    b = jnp.dot(hb, win_ref[:, D_FF:], preferred_element_type=F32)
    act = (a * jax.nn.sigmoid(a) * b).astype(BF16)
    y = jnp.dot(act, wout_ref[...], preferred_element_type=F32)
    xn = x + (MACARON_W * gate) * y
    if final_norm:
        xn = (xn * _rms_scale(xn)) * gf_ref[...]
    o_ref[...] = xn


def _resident(shape):
    zeros = (0,) * len(shape)
    return pl.BlockSpec(shape, lambda b, i: zeros, pipeline_mode=pl.Buffered(1))


def _ffn_call(x, ada4, sub, g, w_in, w_out, g_final, name):
    bsz, s, d = x.shape
    tm = TM_FFN
    final_norm = g_final is not None
    tile = pl.BlockSpec((None, tm, d), lambda b, i: (b, i, 0))
    in_specs = [
        tile,
        pl.BlockSpec((None, None, N_SUBLAYERS, d), lambda b, i: (b, sub, 0, 0)),
        _resident((1, d)),
        _resident(w_in.shape),
        _resident(w_out.shape),
    ]
    args = [x, ada4, g.reshape(1, d), w_in, w_out]
    if final_norm:
        in_specs.append(_resident((1, d)))
        args.append(g_final.reshape(1, d))
    return pl.pallas_call(
        functools.partial(_ffn_kernel, final_norm=final_norm),
        out_shape=jax.ShapeDtypeStruct(x.shape, F32),
        grid=(bsz, s // tm),
        in_specs=in_specs,
        out_specs=tile,
        compiler_params=pltpu.CompilerParams(
            dimension_semantics=("arbitrary", "arbitrary"),
            vmem_limit_bytes=VMEM_LIMIT_BYTES),
        name=name,
    )(*args)


def _mix_kernel(x_ref, ada_ref, g_ref, win_ref, convw_ref, lng_ref, lnb_ref,
                wsp_ref, bsp_ref, wa_ref, wb_ref, wo_ref, o_ref,
                zbuf_ref, wspm_ref):
    d = D_MODEL
    tm = x_ref.shape[0]
    halo = V7X_SUBLANES
    first_tile = pl.program_id(1) == 0

    @pl.when(jnp.logical_and(pl.program_id(0) == 0, first_tile))
    def _():
        t_idx = lax.broadcasted_iota(jnp.int32, (CHUNK, CHUNK), 0)
        s_idx = lax.broadcasted_iota(jnp.int32, (CHUNK, CHUNK), 1)
        causal = (s_idx <= t_idx)[None]
        wspm_ref[...] = jnp.where(causal, wsp_ref[...], 0.0).astype(BF16)

    @pl.when(first_tile)
    def _():
        zbuf_ref[0:halo, :] = jnp.zeros((halo, d), F32)

    x = x_ref[...]
    ada = ada_ref[...]
    shift, scale, gate = ada[0:1], ada[1:2], ada[2:3]
    h = (x * _rms_scale(x)) * (g_ref[...] * (1.0 + scale)) + shift
    hb = h.astype(BF16)

    def proj(k):
        return jnp.dot(hb, win_ref[:, k * d:(k + 1) * d],
                       preferred_element_type=F32)

    gv = jax.nn.gelu(proj(6), approximate=True)
    mu = jnp.mean(gv, axis=-1, keepdims=True)
    cen = gv - mu
    var = jnp.mean(cen * cen, axis=-1, keepdims=True)
    v = (cen * lax.rsqrt(var + EPS)) * lng_ref[...] + lnb_ref[...]
    vb = v.astype(BF16)

    z = proj(3) * proj(4)
    zbuf_ref[halo:halo + tm, :] = z
    cw = convw_ref[...]
    conv = (cw[0:1] * zbuf_ref[halo - 2:halo - 2 + tm, :]
            + cw[1:2] * zbuf_ref[halo - 1:halo - 1 + tm, :]
            + cw[2:3] * z)
    zbuf_ref[0:halo, :] = zbuf_ref[tm:tm + halo, :]
    y_a = jnp.dot((proj(2) * conv).astype(BF16), wa_ref[...],
                  preferred_element_type=F32)

    u = jax.nn.gelu(proj(5), approximate=True)
    bias = bsp_ref[...]
    sv_rows = []
    for c in range(tm // CHUNK):
        r0 = c * CHUNK
        heads = []
        for hd in range(N_HEADS_B):
            c0 = hd * DH_B
            heads.append(jnp.dot(wspm_ref[hd], vb[r0:r0 + CHUNK, c0:c0 + DH_B],
                                 preferred_element_type=F32))
        sv_rows.append(jnp.concatenate(heads, axis=1) + bias)
    sv = jnp.concatenate(sv_rows, axis=0)
    gate_a = jax.nn.sigmoid(proj(0))
    gate_b = jax.nn.sigmoid(proj(1))
    y_b = jnp.dot((u * sv).astype(BF16), wb_ref[...],
                  preferred_element_type=F32)

    merged = gate_a * y_a + gate_b * y_b
    m = jnp.dot(merged.astype(BF16), wo_ref[...], preferred_element_type=F32)
    o_ref[...] = x + gate * m


def _mix_call(x, ada4, g, w_in, conv_w, ln_g, ln_b, w_sp, b_sp_full,
              w_a, w_b, w_o):
    bsz, s, d = x.shape
    tm = TM_MIX
    tile = pl.BlockSpec((None, tm, d), lambda b, i: (b, i, 0))
    in_specs = [
        tile,
        pl.BlockSpec((None, None, N_SUBLAYERS, d), lambda b, i: (b, 1, 0, 0)),
        _resident((1, d)),
        _resident(w_in.shape),
        _resident(conv_w.shape),
        _resident((1, d)),
        _resident((1, d)),
        _resident(w_sp.shape),
        _resident(b_sp_full.shape),
        _resident(w_a.shape),
        _resident(w_b.shape),
        _resident(w_o.shape),
    ]
    return pl.pallas_call(
        _mix_kernel,
        out_shape=jax.ShapeDtypeStruct(x.shape, F32),
        grid=(bsz, s // tm),
        in_specs=in_specs,
        out_specs=tile,
        scratch_shapes=[
            pltpu.VMEM((tm + V7X_SUBLANES, d), F32),
            pltpu.VMEM((N_HEADS_B, CHUNK, CHUNK), BF16),
        ],
        compiler_params=pltpu.CompilerParams(
            dimension_semantics=("arbitrary", "arbitrary"),
            vmem_limit_bytes=VMEM_LIMIT_BYTES),
        name="mix",
    )(x, ada4, g.reshape(1, d), w_in, conv_w, ln_g.reshape(1, d),
      ln_b.reshape(1, d), w_sp, b_sp_full, w_a, w_b, w_o)


def kernel(x, c, w_ada, b_ada, g_ffn1, w_ffn1_in, w_ffn1_out, g_mix, w_mix_in,
           conv_w, ln_v_g, ln_v_b, w_spatial, b_spatial, w_a_out, w_b_out,
           w_mix_out, g_ffn2, w_ffn2_in, w_ffn2_out, g_final):
    bsz, _, d = x.shape
    ada = _ada_call(c, w_ada, b_ada)
    ada4 = ada.reshape(bsz, N_SUBLAYERS, 3, d)
    b_sp_full = jnp.repeat(b_spatial.T, DH_B, axis=1)
    bf = lambda w: w.astype(BF16)

    x = _ffn_call(x, ada4, 0, g_ffn1, bf(w_ffn1_in), bf(w_ffn1_out), None,
                  "ffn1")
    x = _mix_call(x, ada4, g_mix, bf(w_mix_in), conv_w, ln_v_g, ln_v_b,
                  w_spatial, b_sp_full, bf(w_a_out), bf(w_b_out), bf(w_mix_out))
    x = _ffn_call(x, ada4, 2, g_ffn2, bf(w_ffn2_in), bf(w_ffn2_out), g_final,
                  "ffn2_final")
    return x
```

```python
import functools

import jax
import jax.numpy as jnp
from jax import lax
from jax.experimental import pallas as pl
from jax.experimental.pallas import tpu as pltpu

D_MODEL = 1024
D_FF = 2816
N_HEADS_B = 8
DH_B = 128
CHUNK = 128
CONV_K = 3
N_SUBLAYERS = 3
EPS = 1e-6
MACARON_W = 0.5

F32 = jnp.float32
BF16 = jnp.bfloat16

V7X_SUBLANES = 8
V7X_LANES = 128
V7X_BF16_ROWS_PER_VREG = 16
V7X_VMEM_BYTES = 64 * 1024 * 1024
VMEM_LIMIT_BYTES = V7X_VMEM_BYTES - 8 * 1024 * 1024

TM_FFN = 512
TM_MIX = 512
ADA_BLOCK_N = 768
CAST_CHUNK_BYTES = 1024 * 1024


def _rms_scale(x):
    return lax.rsqrt(jnp.mean(x * x, axis=-1, keepdims=True) + EPS)


def _is_first_step():
    return jnp.logical_and(pl.program_id(0) == 0, pl.program_id(1) == 0)


def _cast_chunk_rows(k, n):
    budget = CAST_CHUNK_BYTES // (n * 4)
    rows = [r for r in range(V7X_BF16_ROWS_PER_VREG, k + 1, V7X_BF16_ROWS_PER_VREG)
            if k % r == 0 and r <= budget]
    return max(rows)


def _stage_spec(k, n):
    return pltpu.VMEM((2, _cast_chunk_rows(k, n), n), F32)


def _load_cast(w_hbm, w_vmem, stage, sem):
    chunk = stage.shape[1]
    n_chunks = w_hbm.shape[0] // chunk

    def copy(c, slot):
        return pltpu.make_async_copy(
            w_hbm.at[pl.ds(c * chunk, chunk), :], stage.at[slot], sem.at[slot])

    copy(0, 0).start()

    def body(c, carry):
        slot = lax.rem(c, 2)

        @pl.when(c + 1 < n_chunks)
        def _():
            copy(c + 1, 1 - slot).start()

        copy(c, slot).wait()
        row = pl.multiple_of(c * chunk, chunk)
        w_vmem[pl.ds(row, chunk), :] = stage[slot].astype(BF16)
        return carry

    lax.fori_loop(0, n_chunks, body, 0)


def _ada_kernel(ct_ref, w_ref, b_ref, o_ref, s_ref, *, batch):
    d = w_ref.shape[0]
    nblk = w_ref.shape[1]
    n_k = d // V7X_SUBLANES

    @pl.when(pl.program_id(0) == 0)
    def _():
        ct = ct_ref[...]
        s = ct * jax.nn.sigmoid(ct)
        for b in range(batch):
            s_ref[b] = jnp.broadcast_to(s[:, b:b + 1], (d, V7X_LANES))

    def body(k, accs):
        row = pl.multiple_of(k * V7X_SUBLANES, V7X_SUBLANES)
        w8 = w_ref[pl.ds(row, V7X_SUBLANES), :]
        out = []
        for b, acc in enumerate(accs):
            s8 = s_ref[b, pl.ds(row, V7X_SUBLANES), :]
            out.append(acc + w8 * jnp.tile(s8, (1, nblk // V7X_LANES)))
        return tuple(out)

    init = tuple(jnp.zeros((V7X_SUBLANES, nblk), F32) for _ in range(batch))
    accs = lax.fori_loop(0, n_k, body, init, unroll=8)
    rows = [jnp.sum(acc, axis=0, keepdims=True) for acc in accs]
    o_ref[...] = jnp.concatenate(rows, axis=0) + b_ref[...]


def _ada_call(c, w_ada, b_ada):
    batch, d = c.shape
    n = w_ada.shape[1]
    return pl.pallas_call(
        functools.partial(_ada_kernel, batch=batch),
        out_shape=jax.ShapeDtypeStruct((batch, n), F32),
        grid=(n // ADA_BLOCK_N,),
        in_specs=[
            pl.BlockSpec((d, batch), lambda j: (0, 0)),
            pl.BlockSpec((d, ADA_BLOCK_N), lambda j: (0, j)),
            pl.BlockSpec((1, ADA_BLOCK_N), lambda j: (0, j)),
        ],
        out_specs=pl.BlockSpec((batch, ADA_BLOCK_N), lambda j: (0, j)),
        scratch_shapes=[pltpu.VMEM((batch, d, V7X_LANES), F32)],
        compiler_params=pltpu.CompilerParams(
            dimension_semantics=("arbitrary",),
            vmem_limit_bytes=VMEM_LIMIT_BYTES),
        name="ada_proj",
    )(c.T, w_ada, b_ada.reshape(1, n))


def _ffn_kernel(x_ref, ada_ref, g_ref, gf_ref, win_hbm, wout_hbm, o_ref,
                win_ref, wout_ref, stage_in, stage_out, sem, *, final_norm):
    @pl.when(_is_first_step())
    def _():
        _load_cast(win_hbm, win_ref, stage_in, sem)
        _load_cast(wout_hbm, wout_ref, stage_out, sem)

    x = x_ref[...]
    ada = ada_ref[...]
    shift, scale, gate = ada[0:1], ada[1:2], ada[2:3]
    h = (x * _rms_scale(x)) * (g_ref[...] * (1.0 + scale)) + shift
    hb = h.astype(BF16)
    a = jnp.dot(hb, win_ref[:, :D_FF], preferred_element_type=F32)
    b = jnp.dot(hb, win_ref[:, D_FF:], preferred_element_type=F32)
    act = (a * jax.nn.sigmoid(a) * b).astype(BF16)
    y = jnp.dot(act, wout_ref[...], preferred_element_type=F32)
    xn = x + (MACARON_W * gate) * y
    if final_norm:
        xn = (xn * _rms_scale(xn)) * gf_ref[...]
    o_ref[...] = xn


def _resident(shape):
    zeros = (0,) * len(shape)
    return pl.BlockSpec(shape, lambda b, i: zeros, pipeline_mode=pl.Buffered(1))


_HBM = pl.BlockSpec(memory_space=pl.ANY)


def _ffn_call(x, ada4, sub, g, g_final, w_in, w_out, final_norm, name):
    bsz, s, d = x.shape
    tm = TM_FFN
    tile = pl.BlockSpec((None, tm, d), lambda b, i: (b, i, 0))
    in_specs = [
        tile,
        pl.BlockSpec((None, None, N_SUBLAYERS, d), lambda b, i: (b, sub, 0, 0)),
        _resident((1, d)),
        _resident((1, d)),
        _HBM,
        _HBM,
    ]
    return pl.pallas_call(
        functools.partial(_ffn_kernel, final_norm=final_norm),
        out_shape=jax.ShapeDtypeStruct(x.shape, F32),
        grid=(bsz, s // tm),
        in_specs=in_specs,
        out_specs=tile,
        scratch_shapes=[
            pltpu.VMEM(w_in.shape, BF16),
            pltpu.VMEM(w_out.shape, BF16),
            _stage_spec(*w_in.shape),
            _stage_spec(*w_out.shape),
            pltpu.SemaphoreType.DMA((2,)),
        ],
        compiler_params=pltpu.CompilerParams(
            dimension_semantics=("arbitrary", "arbitrary"),
            vmem_limit_bytes=VMEM_LIMIT_BYTES),
        name=name,
    )(x, ada4, g.reshape(1, d), g_final.reshape(1, d), w_in, w_out)


def _mix_kernel(x_ref, ada_ref, g_ref, convw_ref, lng_ref, lnb_ref, wsp_ref,
                bsp_ref, win_hbm, wa_hbm, wb_hbm, wo_hbm, o_ref,
                win_ref, wa_ref, wb_ref, wo_ref, stage_in, stage_sq, sem,
                zbuf_ref, wspm_ref):
    d = D_MODEL
    tm = x_ref.shape[0]
    halo = V7X_SUBLANES

    @pl.when(_is_first_step())
    def _():
        _load_cast(win_hbm, win_ref, stage_in, sem)
        _load_cast(wa_hbm, wa_ref, stage_sq, sem)
        _load_cast(wb_hbm, wb_ref, stage_sq, sem)
        _load_cast(wo_hbm, wo_ref, stage_sq, sem)
        t_idx = lax.broadcasted_iota(jnp.int32, (CHUNK, CHUNK), 0)
        s_idx = lax.broadcasted_iota(jnp.int32, (CHUNK, CHUNK), 1)
        causal = (s_idx <= t_idx)[None]
        wspm_ref[...] = jnp.where(causal, wsp_ref[...], 0.0).astype(BF16)

    @pl.when(pl.program_id(1) == 0)
    def _():
        zbuf_ref[0:halo, :] = jnp.zeros((halo, d), F32)

    x = x_ref[...]
    ada = ada_ref[...]
    shift, scale, gate = ada[0:1], ada[1:2], ada[2:3]
    h = (x * _rms_scale(x)) * (g_ref[...] * (1.0 + scale)) + shift
    hb = h.astype(BF16)

    def proj(k):
        return jnp.dot(hb, win_ref[:, k * d:(k + 1) * d],
                       preferred_element_type=F32)

    gv = jax.nn.gelu(proj(6), approximate=True)
    mu = jnp.mean(gv, axis=-1, keepdims=True)
    cen = gv - mu
    var = jnp.mean(cen * cen, axis=-1, keepdims=True)
    v = (cen * lax.rsqrt(var + EPS)) * lng_ref[...] + lnb_ref[...]
    vb = v.astype(BF16)

    z = proj(3) * proj(4)
    zbuf_ref[halo:halo + tm, :] = z
    cw = convw_ref[...]
    conv = (cw[0:1] * zbuf_ref[halo - 2:halo - 2 + tm, :]
            + cw[1:2] * zbuf_ref[halo - 1:halo - 1 + tm, :]
            + cw[2:3] * z)
    zbuf_ref[0:halo, :] = zbuf_ref[tm:tm + halo, :]
    y_a = jnp.dot((proj(2) * conv).astype(BF16), wa_ref[...],
                  preferred_element_type=F32)

    u = jax.nn.gelu(proj(5), approximate=True)
    bias = bsp_ref[...]
    sv_rows = []
    for c in range(tm // CHUNK):
        r0 = c * CHUNK
        heads = []
        for hd in range(N_HEADS_B):
            c0 = hd * DH_B
            heads.append(jnp.dot(wspm_ref[hd], vb[r0:r0 + CHUNK, c0:c0 + DH_B],
                                 preferred_element_type=F32))
        sv_rows.append(jnp.concatenate(heads, axis=1) + bias)
    sv = jnp.concatenate(sv_rows, axis=0)
    gate_a = jax.nn.sigmoid(proj(0))
    gate_b = jax.nn.sigmoid(proj(1))
    y_b = jnp.dot((u * sv).astype(BF16), wb_ref[...],
                  preferred_element_type=F32)

    merged = gate_a * y_a + gate_b * y_b
    m = jnp.dot(merged.astype(BF16), wo_ref[...], preferred_element_type=F32)
    o_ref[...] = x + gate * m


def _mix_call(x, ada4, g, conv_w, ln_g, ln_b, w_sp, b_sp_full,
              w_in, w_a, w_b, w_o):
    bsz, s, d = x.shape
    tm = TM_MIX
    assert w_a.shape == w_b.shape == w_o.shape
    tile = pl.BlockSpec((None, tm, d), lambda b, i: (b, i, 0))
    in_specs = [
        tile,
        pl.BlockSpec((None, None, N_SUBLAYERS, d), lambda b, i: (b, 1, 0, 0)),
        _resident((1, d)),
        _resident(conv_w.shape),
        _resident((1, d)),
        _resident((1, d)),
        _resident(w_sp.shape),
        _resident(b_sp_full.shape),
        _HBM,
        _HBM,
        _HBM,
        _HBM,
    ]
    return pl.pallas_call(
        _mix_kernel,
        out_shape=jax.ShapeDtypeStruct(x.shape, F32),
        grid=(bsz, s // tm),
        in_specs=in_specs,
        out_specs=tile,
        scratch_shapes=[
            pltpu.VMEM(w_in.shape, BF16),
            pltpu.VMEM(w_a.shape, BF16),
            pltpu.VMEM(w_b.shape, BF16),
            pltpu.VMEM(w_o.shape, BF16),
            _stage_spec(*w_in.shape),
            _stage_spec(*w_a.shape),
            pltpu.SemaphoreType.DMA((2,)),
            pltpu.VMEM((tm + V7X_SUBLANES, d), F32),
            pltpu.VMEM((N_HEADS_B, CHUNK, CHUNK), BF16),
        ],
        compiler_params=pltpu.CompilerParams(
            dimension_semantics=("arbitrary", "arbitrary"),
            vmem_limit_bytes=VMEM_LIMIT_BYTES),
        name="mix",
    )(x, ada4, g.reshape(1, d), conv_w, ln_g.reshape(1, d), ln_b.reshape(1, d),
      w_sp, b_sp_full, w_in, w_a, w_b, w_o)


def kernel(x, c, w_ada, b_ada, g_ffn1, w_ffn1_in, w_ffn1_out, g_mix, w_mix_in,
           conv_w, ln_v_g, ln_v_b, w_spatial, b_spatial, w_a_out, w_b_out,
           w_mix_out, g_ffn2, w_ffn2_in, w_ffn2_out, g_final):
    bsz, _, d = x.shape
    ada = _ada_call(c, w_ada, b_ada)
    ada4 = ada.reshape(bsz, N_SUBLAYERS, 3, d)
    b_sp_full = jnp.repeat(b_spatial.T, DH_B, axis=1)

    x = _ffn_call(x, ada4, 0, g_ffn1, g_final, w_ffn1_in, w_ffn1_out, False,
                  "ffn1")
    x = _mix_call(x, ada4, g_mix, conv_w, ln_v_g, ln_v_b, w_spatial, b_sp_full,
                  w_mix_in, w_a_out, w_b_out, w_mix_out)
    x = _ffn_call(x, ada4, 2, g_ffn2, g_final, w_ffn2_in, w_ffn2_out, True,
                  "ffn2_final")
    return x
```

```python
import functools

import jax
import jax.numpy as jnp
from jax import lax
from jax.experimental import pallas as pl
from jax.experimental.pallas import tpu as pltpu

D_MODEL = 1024
D_FF = 2816
N_HEADS_B = 8
DH_B = 128
CHUNK = 128
CONV_K = 3
N_SUBLAYERS = 3
EPS = 1e-6
MACARON_W = 0.5

F32 = jnp.float32
BF16 = jnp.bfloat16

V7X_SUBLANES = 8
V7X_LANES = 128
V7X_BF16_ROWS_PER_VREG = 16
V7X_VMEM_BYTES = 64 * 1024 * 1024
VMEM_LIMIT_BYTES = V7X_VMEM_BYTES - 8 * 1024 * 1024

TM_FFN = 1024
FFN_SUBTILES = 2
TM_MIX = 512
ADA_BLOCK_K = 64
ADA_COLS = 1024
CAST_CHUNK_BYTES = 2 * 1024 * 1024
CAST_RING_SLOTS = 3


def _rms_scale(x):
    return lax.rsqrt(jnp.mean(x * x, axis=-1, keepdims=True) + EPS)


def _is_first_step():
    return jnp.logical_and(pl.program_id(0) == 0, pl.program_id(1) == 0)


def _cast_chunk_rows(k, n):
    budget = CAST_CHUNK_BYTES // (n * 4)
    rows = [r for r in range(V7X_BF16_ROWS_PER_VREG, k + 1, V7X_BF16_ROWS_PER_VREG)
            if k % r == 0 and r <= budget]
    return max(rows)


def _ring_scratch(k, n):
    rows = _cast_chunk_rows(k, n)
    assert k // rows >= CAST_RING_SLOTS - 1
    return [pltpu.VMEM((CAST_RING_SLOTS, rows, n), F32),
            pltpu.SemaphoreType.DMA((CAST_RING_SLOTS,))]


def _chunk_copy(w_hbm, stage, sem, c, g):
    slots, rows = stage.shape[0], stage.shape[1]
    slot = g % slots if isinstance(g, int) else lax.rem(g, slots)
    return pltpu.make_async_copy(
        w_hbm.at[pl.ds(c * rows, rows), :], stage.at[slot], sem.at[slot])


def _ring_start(w_hbm, stage, sem, g0):
    for c in range(stage.shape[0] - 1):
        _chunk_copy(w_hbm, stage, sem, c, g0 + c).start()


def _ring_finish(w_hbm, w_vmem, stage, sem, g0, next_hbm=None):
    slots, rows = stage.shape[0], stage.shape[1]
    n_chunks = w_hbm.shape[0] // rows

    def body(c, carry):
        ahead = c + slots - 1

        @pl.when(ahead < n_chunks)
        def _():
            _chunk_copy(w_hbm, stage, sem, ahead, g0 + ahead).start()

        if next_hbm is not None:
            @pl.when(ahead >= n_chunks)
            def _():
                _chunk_copy(next_hbm, stage, sem, ahead - n_chunks,
                            g0 + ahead).start()

        _chunk_copy(w_hbm, stage, sem, c, g0 + c).wait()
        row = pl.multiple_of(c * rows, rows)
        w_vmem[pl.ds(row, rows), :] = stage[lax.rem(g0 + c, slots)].astype(BF16)
        return carry

    lax.fori_loop(0, n_chunks, body, 0)
    return g0 + n_chunks


def _ada_kernel(ct_ref, w_ref, b_ref, o_ref, s_ref, acc_ref, *, batch):
    j = pl.program_id(0)
    d = ct_ref.shape[0]
    rows, n = w_ref.shape
    sl = V7X_SUBLANES

    @pl.when(j == 0)
    def _():
        ct = ct_ref[...]
        s = ct * jax.nn.sigmoid(ct)
        for b in range(batch):
            s_ref[b] = jnp.broadcast_to(s[:, b:b + 1], (d, V7X_LANES))
        acc_ref[...] = jnp.zeros(acc_ref.shape, F32)

    base = pl.multiple_of(j * rows, rows)
    s_blocks = [[s_ref[b, pl.ds(base + kb * sl, sl), :]
                 for kb in range(rows // sl)] for b in range(batch)]
    for c0 in range(0, n, ADA_COLS):
        accs = [acc_ref[b, :, c0:c0 + ADA_COLS] for b in range(batch)]
        for kb in range(rows // sl):
            w8 = w_ref[kb * sl:(kb + 1) * sl, c0:c0 + ADA_COLS]
            for b in range(batch):
                accs[b] = accs[b] + w8 * jnp.tile(s_blocks[b][kb],
                                                  (1, ADA_COLS // V7X_LANES))
        for b in range(batch):
            acc_ref[b, :, c0:c0 + ADA_COLS] = accs[b]

    @pl.when(j == pl.num_programs(0) - 1)
    def _():
        sums = [jnp.sum(acc_ref[b], axis=0, keepdims=True) for b in range(batch)]
        o_ref[...] = jnp.concatenate(sums, axis=0) + b_ref[...]


def _ada_call(c, w_ada, b_ada):
    batch, d = c.shape
    n = w_ada.shape[1]
    return pl.pallas_call(
        functools.partial(_ada_kernel, batch=batch),
        out_shape=jax.ShapeDtypeStruct((batch, n), F32),
        grid=(d // ADA_BLOCK_K,),
        in_specs=[
            pl.BlockSpec((d, batch), lambda j: (0, 0)),
            pl.BlockSpec((ADA_BLOCK_K, n), lambda j: (j, 0)),
            pl.BlockSpec((1, n), lambda j: (0, 0)),
        ],
        out_specs=pl.BlockSpec((batch, n), lambda j: (0, 0)),
        scratch_shapes=[pltpu.VMEM((batch, d, V7X_LANES), F32),
                        pltpu.VMEM((batch, V7X_SUBLANES, n), F32)],
        compiler_params=pltpu.CompilerParams(
            dimension_semantics=("arbitrary",),
            vmem_limit_bytes=VMEM_LIMIT_BYTES),
        name="ada_proj",
    )(c.T, w_ada, b_ada.reshape(1, n))


def _ffn_kernel(x_ref, ada_ref, g_ref, gf_ref, win_hbm, wout_hbm, o_ref,
                win_ref, wout_ref, stage_in, sem_in, stage_out, sem_out, *,
                final_norm):
    @pl.when(_is_first_step())
    def _():
        _ring_start(win_hbm, stage_in, sem_in, 0)
        _ring_start(wout_hbm, stage_out, sem_out, 0)
        _ring_finish(win_hbm, win_ref, stage_in, sem_in, 0)
        _ring_finish(wout_hbm, wout_ref, stage_out, sem_out, 0)

    ada = ada_ref[...]
    shift, scale, gate = ada[0:1], ada[1:2], ada[2:3]
    in_gain = g_ref[...] * (1.0 + scale)
    out_gate = MACARON_W * gate
    sub = x_ref.shape[0] // FFN_SUBTILES
    for s in range(FFN_SUBTILES):
        rows = pl.ds(s * sub, sub)
        x = x_ref[rows, :]
        hb = ((x * _rms_scale(x)) * in_gain + shift).astype(BF16)
        a = jnp.dot(hb, win_ref[:, :D_FF], preferred_element_type=F32)
        b = jnp.dot(hb, win_ref[:, D_FF:], preferred_element_type=F32)
        act = (a * jax.nn.sigmoid(a) * b).astype(BF16)
        y = jnp.dot(act, wout_ref[...], preferred_element_type=F32)
        xn = x + out_gate * y
        if final_norm:
            xn = (xn * _rms_scale(xn)) * gf_ref[...]
        o_ref[rows, :] = xn


def _resident(shape):
    zeros = (0,) * len(shape)
    return pl.BlockSpec(shape, lambda b, i: zeros, pipeline_mode=pl.Buffered(1))


_HBM = pl.BlockSpec(memory_space=pl.ANY)


def _ffn_call(x, ada4, sub, g, g_final, w_in, w_out, final_norm, name):
    bsz, s, d = x.shape
    tm = TM_FFN
    tile = pl.BlockSpec((None, tm, d), lambda b, i: (b, i, 0))
    in_specs = [
        tile,
        pl.BlockSpec((None, None, N_SUBLAYERS, d), lambda b, i: (b, sub, 0, 0)),
        _resident((1, d)),
        _resident((1, d)),
        _HBM,
        _HBM,
    ]
    return pl.pallas_call(
        functools.partial(_ffn_kernel, final_norm=final_norm),
        out_shape=jax.ShapeDtypeStruct(x.shape, F32),
        grid=(bsz, s // tm),
        in_specs=in_specs,
        out_specs=tile,
        scratch_shapes=[
            pltpu.VMEM(w_in.shape, BF16),
            pltpu.VMEM(w_out.shape, BF16),
            *_ring_scratch(*w_in.shape),
            *_ring_scratch(*w_out.shape),
        ],
        compiler_params=pltpu.CompilerParams(
            dimension_semantics=("arbitrary", "arbitrary"),
            vmem_limit_bytes=VMEM_LIMIT_BYTES),
        name=name,
    )(x, ada4, g.reshape(1, d), g_final.reshape(1, d), w_in, w_out)


def _mix_kernel(x_ref, ada_ref, g_ref, convw_ref, lng_ref, lnb_ref, wsp_ref,
                bsp_ref, win_hbm, wa_hbm, wb_hbm, wo_hbm, o_ref,
                win_ref, wa_ref, wb_ref, wo_ref, stage_in, sem_in, stage_sq,
                sem_sq, zbuf_ref, wspm_ref):
    d = D_MODEL
    tm = x_ref.shape[0]
    halo = V7X_SUBLANES

    @pl.when(_is_first_step())
    def _():
        _ring_start(win_hbm, stage_in, sem_in, 0)
        _ring_start(wa_hbm, stage_sq, sem_sq, 0)
        _ring_finish(win_hbm, win_ref, stage_in, sem_in, 0)
        g = _ring_finish(wa_hbm, wa_ref, stage_sq, sem_sq, 0, next_hbm=wb_hbm)
        g = _ring_finish(wb_hbm, wb_ref, stage_sq, sem_sq, g, next_hbm=wo_hbm)
        _ring_finish(wo_hbm, wo_ref, stage_sq, sem_sq, g)
        t_idx = lax.broadcasted_iota(jnp.int32, (CHUNK, CHUNK), 0)
        s_idx = lax.broadcasted_iota(jnp.int32, (CHUNK, CHUNK), 1)
        causal = (s_idx <= t_idx)[None]
        wspm_ref[...] = jnp.where(causal, wsp_ref[...], 0.0).astype(BF16)

    @pl.when(pl.program_id(1) == 0)
    def _():
        zbuf_ref[0:halo, :] = jnp.zeros((halo, d), F32)

    x = x_ref[...]
    ada = ada_ref[...]
    shift, scale, gate = ada[0:1], ada[1:2], ada[2:3]
    h = (x * _rms_scale(x)) * (g_ref[...] * (1.0 + scale)) + shift
    hb = h.astype(BF16)

    def proj(k):
        return jnp.dot(hb, win_ref[:, k * d:(k + 1) * d],
                       preferred_element_type=F32)

    gv = jax.nn.gelu(proj(6), approximate=True)
    mu = jnp.mean(gv, axis=-1, keepdims=True)
    cen = gv - mu
    var = jnp.mean(cen * cen, axis=-1, keepdims=True)
    v = (cen * lax.rsqrt(var + EPS)) * lng_ref[...] + lnb_ref[...]
    vb = v.astype(BF16)

    z = proj(3) * proj(4)
    zbuf_ref[halo:halo + tm, :] = z
    cw = convw_ref[...]
    conv = (cw[0:1] * zbuf_ref[halo - 2:halo - 2 + tm, :]
            + cw[1:2] * zbuf_ref[halo - 1:halo - 1 + tm, :]
            + cw[2:3] * z)
    zbuf_ref[0:halo, :] = zbuf_ref[tm:tm + halo, :]
    y_a = jnp.dot((proj(2) * conv).astype(BF16), wa_ref[...],
                  preferred_element_type=F32)

    u = jax.nn.gelu(proj(5), approximate=True)
    bias = bsp_ref[...]
    sv_rows = []
    for c in range(tm // CHUNK):
        r0 = c * CHUNK
        heads = []
        for hd in range(N_HEADS_B):
            c0 = hd * DH_B
            heads.append(jnp.dot(wspm_ref[hd], vb[r0:r0 + CHUNK, c0:c0 + DH_B],
                                 preferred_element_type=F32))
        sv_rows.append(jnp.concatenate(heads, axis=1) + bias)
    sv = jnp.concatenate(sv_rows, axis=0)
    gate_a = jax.nn.sigmoid(proj(0))
    gate_b = jax.nn.sigmoid(proj(1))
    y_b = jnp.dot((u * sv).astype(BF16), wb_ref[...],
                  preferred_element_type=F32)

    merged = gate_a * y_a + gate_b * y_b
    m = jnp.dot(merged.astype(BF16), wo_ref[...], preferred_element_type=F32)
    o_ref[...] = x + gate * m


def _mix_call(x, ada4, g, conv_w, ln_g, ln_b, w_sp, b_sp_full,
              w_in, w_a, w_b, w_o):
    bsz, s, d = x.shape
    tm = TM_MIX
    assert w_a.shape == w_b.shape == w_o.shape
    tile = pl.BlockSpec((None, tm, d), lambda b, i: (b, i, 0))
    in_specs = [
        tile,
        pl.BlockSpec((None, None, N_SUBLAYERS, d), lambda b, i: (b, 1, 0, 0)),
        _resident((1, d)),
        _resident(conv_w.shape),
        _resident((1, d)),
        _resident((1, d)),
        _resident(w_sp.shape),
        _resident(b_sp_full.shape),
        _HBM,
        _HBM,
        _HBM,
        _HBM,
    ]
    return pl.pallas_call(
        _mix_kernel,
        out_shape=jax.ShapeDtypeStruct(x.shape, F32),
        grid=(bsz, s // tm),
        in_specs=in_specs,
        out_specs=tile,
        scratch_shapes=[
            pltpu.VMEM(w_in.shape, BF16),
            pltpu.VMEM(w_a.shape, BF16),
            pltpu.VMEM(w_b.shape, BF16),
            pltpu.VMEM(w_o.shape, BF16),
            *_ring_scratch(*w_in.shape),
            *_ring_scratch(*w_a.shape),
            pltpu.VMEM((tm + V7X_SUBLANES, d), F32),
            pltpu.VMEM((N_HEADS_B, CHUNK, CHUNK), BF16),
        ],
        compiler_params=pltpu.CompilerParams(
            dimension_semantics=("arbitrary", "arbitrary"),
            vmem_limit_bytes=VMEM_LIMIT_BYTES),
        name="mix",
    )(x, ada4, g.reshape(1, d), conv_w, ln_g.reshape(1, d), ln_b.reshape(1, d),
      w_sp, b_sp_full, w_in, w_a, w_b, w_o)


def kernel(x, c, w_ada, b_ada, g_ffn1, w_ffn1_in, w_ffn1_out, g_mix, w_mix_in,
           conv_w, ln_v_g, ln_v_b, w_spatial, b_spatial, w_a_out, w_b_out,
           w_mix_out, g_ffn2, w_ffn2_in, w_ffn2_out, g_final):
    bsz, _, d = x.shape
    ada = _ada_call(c, w_ada, b_ada)
    ada4 = ada.reshape(bsz, N_SUBLAYERS, 3, d)
    b_sp_full = jnp.repeat(b_spatial.T, DH_B, axis=1)

    x = _ffn_call(x, ada4, 0, g_ffn1, g_final, w_ffn1_in, w_ffn1_out, False,
                  "ffn1")
    x = _mix_call(x, ada4, g_mix, conv_w, ln_v_g, ln_v_b, w_spatial, b_sp_full,
                  w_mix_in, w_a_out, w_b_out, w_mix_out)
    x = _ffn_call(x, ada4, 2, g_ffn2, g_final, w_ffn2_in, w_ffn2_out, True,
                  "ffn2_final")
    return x
```
